```python
import math
import jax
import jax.numpy as jnp
from jax import lax
import numpy as np

D_MODEL = 1024
BATCH = 8
SEQ = 2048
DEPTH = 1
DEC_BATCH = 128
DEC_SEQ = 8
PAST_LEN = 8192
PAGE_SIZE = 128

NORM_EPS = 1e-6
SSM_D_INNER = 2 * D_MODEL
SSM_HEAD_DIM = 64
SSM_N_HEADS = SSM_D_INNER // SSM_HEAD_DIM
SSM_N_GROUPS = 8
SSM_D_STATE = 128
SSM_CONV_WIDTH = 4
SSM_CONV_DIM = SSM_D_INNER + 2 * SSM_N_GROUPS * SSM_D_STATE
SSM_BC_WIDTH = SSM_N_GROUPS * SSM_D_STATE
SSM_CHUNK = 128
ATTN_HEAD_DIM = 64
ATTN_N_HEADS = D_MODEL // ATTN_HEAD_DIM
ATTN_N_KV_HEADS = 4
ATTN_Q_PER_KV = ATTN_N_HEADS // ATTN_N_KV_HEADS
ATTN_Q_WIDTH = ATTN_N_HEADS * ATTN_HEAD_DIM
ATTN_KV_WIDTH = ATTN_N_KV_HEADS * ATTN_HEAD_DIM
WINDOW = 128
ROPE_THETA = 10000.0
PEER_N_KEYS = 128
PEER_N_EXPERTS = PEER_N_KEYS * PEER_N_KEYS
PEER_HEADS = 8
PEER_QUERY_DIM = 128
PEER_HALF_DIM = PEER_QUERY_DIM // 2
PEER_TOPK = 16
PEER_BLOCK = 128
IN_SPLIT_SIZES = (SSM_D_INNER, SSM_CONV_DIM, SSM_N_HEADS, ATTN_Q_WIDTH, ATTN_KV_WIDTH, ATTN_KV_WIDTH, D_MODEL, D_MODEL)
IN_WIDTH = SSM_D_INNER + SSM_CONV_DIM + SSM_N_HEADS + ATTN_Q_WIDTH + 2 * ATTN_KV_WIDTH + 2 * D_MODEL

kernel_name = 'hybrid_ssd_swa_sink_peer_step'


def rms_norm(x, w):
    xf = x.astype(jnp.float32)
    y = xf * lax.rsqrt(jnp.mean(xf * xf, axis=-1, keepdims=True) + NORM_EPS)
    return (y * w.astype(jnp.float32)).astype(x.dtype)


def split_in_proj(proj):
    offsets = []
    acc = 0
    for size in IN_SPLIT_SIZES[:-1]:
        acc += size
        offsets.append(acc)
    return jnp.split(proj, offsets, axis=-1)


def rope(x, pos):
    half = x.shape[-1] // 2
    inv = ROPE_THETA ** (-jnp.arange(half, dtype=jnp.float32) / half)
    ang = pos.astype(jnp.float32)[:, None] * inv[None, :]
    cos = jnp.cos(ang)[None, :, None, :]
    sin = jnp.sin(ang)[None, :, None, :]
    xf = x.astype(jnp.float32)
    x1, x2 = xf[..., :half], xf[..., half:]
    return jnp.concatenate([x1 * cos - x2 * sin, x2 * cos + x1 * sin], axis=-1).astype(x.dtype)


def causal_dwconv(u, prev, w, b):
    c = u.shape[-1]
    up = jnp.concatenate([prev.astype(u.dtype), u], axis=1)
    y = lax.conv_general_dilated(up, w.astype(u.dtype)[:, None, :], window_strides=(1,), padding='VALID',
                                 dimension_numbers=('NWC', 'WIO', 'NWC'), feature_group_count=c)
    return y + b.astype(u.dtype), up[:, -(SSM_CONV_WIDTH - 1):]


def ssd_scan(x, dt, a, bmat, cmat, init_state, chunk):
    bsz, l, nh, hp = x.shape
    g, n = bmat.shape[2], bmat.shape[3]
    r = nh // g
    c = l // chunk
    f32 = jnp.float32
    xdt = (x.astype(f32) * dt[..., None]).reshape(bsz, c, chunk, g, r, hp)
    a_cum = jnp.cumsum((dt * a).reshape(bsz, c, chunk, g, r), axis=2)
    bc = bmat.astype(f32).reshape(bsz, c, chunk, g, n)
    cc = cmat.astype(f32).reshape(bsz, c, chunk, g, n)
    causal = jnp.tril(jnp.ones((chunk, chunk), dtype=bool))[:, :, None, None]
    seg = a_cum[:, :, :, None] - a_cum[:, :, None, :]
    decay = jnp.exp(jnp.where(causal, seg, -jnp.inf))
    cb = jnp.einsum('bctgn,bcsgn->bctsg', cc, bc)
    y_diag = jnp.einsum('bctsg,bctsgr,bcsgrp->bctgrp', cb, decay, xdt)
    decay_to_end = jnp.exp(a_cum[:, :, -1:] - a_cum)
    chunk_states = jnp.einsum('bctgn,bctgr,bctgrp->bcgrpn', bc, decay_to_end, xdt)
    chunk_decay = jnp.exp(a_cum[:, :, -1])

    def step(s, inp):
        cs, cd = inp
        return s * cd[..., None, None] + cs, s

    final, prev = lax.scan(step, init_state.reshape(bsz, g, r, hp, n),
                           (jnp.moveaxis(chunk_states, 1, 0), jnp.moveaxis(chunk_decay, 1, 0)))
    prev = jnp.moveaxis(prev, 0, 1)
    y_off = jnp.einsum('bctgn,bcgrpn,bctgr->bctgrp', cc, prev, jnp.exp(a_cum))
    y = (y_diag + y_off).reshape(bsz, l, nh, hp)
    return y, final.reshape(bsz, nh, hp, n)


def mamba_branch(z, xbc, dt_raw, conv_prev, ssm_prev, lw):
    bsz, l, _ = xbc.shape
    f32 = jnp.float32
    u, conv_new = causal_dwconv(xbc, conv_prev, lw['conv_w'], lw['conv_b'])
    u = jax.nn.silu(u)
    xs = u[..., :SSM_D_INNER].reshape(bsz, l, SSM_N_HEADS, SSM_HEAD_DIM)
    bm = u[..., SSM_D_INNER:SSM_D_INNER + SSM_BC_WIDTH].reshape(bsz, l, SSM_N_GROUPS, SSM_D_STATE)
    cm = u[..., SSM_D_INNER + SSM_BC_WIDTH:].reshape(bsz, l, SSM_N_GROUPS, SSM_D_STATE)
    dt = jax.nn.softplus(dt_raw.astype(f32) + lw['dt_bias'].astype(f32))
    a = -jnp.exp(lw['a_log'].astype(f32))
    chunk = SSM_CHUNK if l % SSM_CHUNK == 0 else l
    y, ssm_new = ssd_scan(xs, dt, a, bm, cm, ssm_prev.astype(f32), chunk)
    y = y + xs.astype(f32) * lw['d_skip'].astype(f32)[:, None]
    y = y.reshape(bsz, l, SSM_D_INNER) * jax.nn.silu(z.astype(f32))
    yg = y.reshape(bsz, l, SSM_N_GROUPS, SSM_D_INNER // SSM_N_GROUPS)
    yg = yg * lax.rsqrt(jnp.mean(yg * yg, axis=-1, keepdims=True) + NORM_EPS)
    y = (yg.reshape(bsz, l, SSM_D_INNER) * lw['ssm_norm_w'].astype(f32)).astype(z.dtype)
    return y @ lw['w_ssm_out'], conv_new, ssm_new.astype(ssm_prev.dtype)


def sink_softmax(s, sink):
    m = jnp.maximum(jnp.max(s, axis=-1, keepdims=True), sink)
    p = jnp.exp(s - m)
    return p / (jnp.sum(p, axis=-1, keepdims=True) + jnp.exp(sink - m))


def swa_prompt(q, k, v, sinks):
    bsz, l, _, d = q.shape
    nb = l // WINDOW
    f32 = jnp.float32
    qb = q.astype(f32).reshape(bsz, nb, WINDOW, ATTN_N_KV_HEADS, ATTN_Q_PER_KV, d)
    kb = k.astype(f32).reshape(bsz, nb, WINDOW, ATTN_N_KV_HEADS, d)
    vb = v.astype(f32).reshape(bsz, nb, WINDOW, ATTN_N_KV_HEADS, d)
    shift = lambda t: jnp.concatenate([jnp.zeros_like(t[:, :1]), t[:, :-1]], axis=1)
    kk = jnp.concatenate([shift(kb), kb], axis=2)
    vv = jnp.concatenate([shift(vb), vb], axis=2)
    qi = jnp.arange(WINDOW)[:, None]
    ki = jnp.arange(2 * WINDOW)[None, :] - WINDOW
    band = (ki <= qi) & (qi - ki < WINDOW)
    valid = band[None] & ((jnp.arange(nb)[:, None, None] > 0) | (ki[None] >= 0))
    s = jnp.einsum('bnqhrd,bnkhd->bnhrqk', qb, kk) * (d ** -0.5)
    s = jnp.where(valid[None, :, None, None], s, -jnp.inf)
    p = sink_softmax(s, sinks.astype(f32).reshape(1, 1, ATTN_N_KV_HEADS, ATTN_Q_PER_KV, 1, 1))
    o = jnp.einsum('bnhrqk,bnkhd->bnqhrd', p, vv)
    return o.reshape(bsz, l, ATTN_Q_WIDTH).astype(q.dtype)


def swa_sample(q, k, v, k_buf, v_buf, sinks, pos0):
    bsz, l, _, d = q.shape
    wb = k_buf.shape[1]
    f32 = jnp.float32
    kk = jnp.concatenate([k_buf, k.astype(k_buf.dtype)], axis=1)
    vv = jnp.concatenate([v_buf, v.astype(v_buf.dtype)], axis=1)
    qpos = pos0 + jnp.arange(l)
    kpos = pos0 - wb + jnp.arange(wb + l)
    valid = (kpos[None, :] <= qpos[:, None]) & (qpos[:, None] - kpos[None, :] < WINDOW)
    qg = q.astype(f32).reshape(bsz, l, ATTN_N_KV_HEADS, ATTN_Q_PER_KV, d)
    s = jnp.einsum('bqhrd,bkhd->bhrqk', qg, kk.astype(f32)) * (d ** -0.5)
    s = jnp.where(valid, s, -jnp.inf)
    p = sink_softmax(s, sinks.astype(f32).reshape(1, ATTN_N_KV_HEADS, ATTN_Q_PER_KV, 1, 1))
    o = jnp.einsum('bhrqk,bkhd->bqhrd', p, vv.astype(f32))
    return o.reshape(bsz, l, ATTN_Q_WIDTH).astype(q.dtype), kk[:, -wb:], vv[:, -wb:]


def peer_ffn(h, w_q, sub_keys, u_tab, v_tab):
    shape = h.shape
    f32 = jnp.float32
    hf = h.reshape(-1, D_MODEL)
    t = hf.shape[0]
    q = (hf @ w_q).astype(f32).reshape(t, PEER_HEADS, 2, PEER_HALF_DIM)
    s = jnp.einsum('thjd,jnd->thjn', q, sub_keys.astype(f32))
    sv, si = lax.top_k(s, PEER_TOPK)
    comb = (sv[:, :, 0, :, None] + sv[:, :, 1, None, :]).reshape(t, PEER_HEADS, PEER_TOPK * PEER_TOPK)
    cand = (si[:, :, 0, :, None] * PEER_N_KEYS + si[:, :, 1, None, :]).reshape(t, PEER_HEADS, PEER_TOPK * PEER_TOPK)
    best, pick = lax.top_k(comb, PEER_TOPK)
    experts = jnp.take_along_axis(cand, pick, axis=-1)
    gates = jax.nn.softmax(best, axis=-1)
    pad = (-t) % PEER_BLOCK
    nb = (t + pad) // PEER_BLOCK
    hp = jnp.pad(hf, ((0, pad), (0, 0))).reshape(nb, PEER_BLOCK, D_MODEL)
    ep = jnp.pad(experts, ((0, pad), (0, 0), (0, 0))).reshape(nb, PEER_BLOCK, PEER_HEADS, PEER_TOPK)
    gp = jnp.pad(gates, ((0, pad), (0, 0), (0, 0))).reshape(nb, PEER_BLOCK, PEER_HEADS, PEER_TOPK)

    def expert_block(args):
        hb, eb, gb = args
        act = jax.nn.gelu(jnp.einsum('td,thkd->thk', hb, u_tab[eb]).astype(f32), approximate=False)
        return jnp.einsum('thk,thkd->td', (gb * act).astype(hb.dtype), v_tab[eb])

    out = lax.map(expert_block, (hp, ep, gp))
    return out.reshape(-1, D_MODEL)[:t].reshape(shape)


def trunk_layer(x, conv_prev, ssm_prev, k_buf, v_buf, pos0, lw):
    bsz, l, _ = x.shape
    h = rms_norm(x, lw['norm_mix_w'])
    z, xbc, dt_raw, q, k, v, gate_ssm, gate_attn = split_in_proj(h @ lw['w_in'])
    y_ssm, conv_new, ssm_new = mamba_branch(z, xbc, dt_raw, conv_prev, ssm_prev, lw)
    pos = pos0 + jnp.arange(l)
    q = rope(q.reshape(bsz, l, ATTN_N_HEADS, ATTN_HEAD_DIM), pos)
    k = rope(k.reshape(bsz, l, ATTN_N_KV_HEADS, ATTN_HEAD_DIM), pos)
    v = v.reshape(bsz, l, ATTN_N_KV_HEADS, ATTN_HEAD_DIM)
    if k_buf is None:
        o = swa_prompt(q, k, v, lw['attn_sinks'])
        keep = min(WINDOW, l)
        k_new, v_new = k[:, l - keep:], v[:, l - keep:]
    else:
        o, k_new, v_new = swa_sample(q, k, v, k_buf, v_buf, lw['attn_sinks'], pos0)
    y_attn = o @ lw['w_attn_out']
    merged = jax.nn.sigmoid(gate_ssm) * y_ssm + jax.nn.sigmoid(gate_attn) * y_attn
    x = x + merged @ lw['w_out']
    x = x + peer_ffn(rms_norm(x, lw['norm_ffn_w']), lw['peer_w_q'], lw['peer_sub_keys'], lw['peer_u'], lw['peer_v'])
    return x, conv_new, ssm_new, k_new, v_new


def setup_inputs(seed: int = 0) -> dict:
    key = jax.random.key(seed)
    ks = list(jax.random.split(key, 32))
    nrm = lambda i, shape, scale: scale * jax.random.normal(ks[i], shape, jnp.float32)
    wb = min(WINDOW, PAST_LEN)
    dt0 = jnp.exp(jax.random.uniform(ks[10], (DEPTH, SSM_N_HEADS), jnp.float32, math.log(1e-3), math.log(1e-1)))
    return {
        'x_prompt': nrm(0, (BATCH, SEQ, D_MODEL), 1.0),
        'x_sample': nrm(1, (DEC_BATCH, DEC_SEQ, D_MODEL), 1.0),
        'state_conv': nrm(2, (DEPTH, DEC_BATCH, SSM_CONV_WIDTH - 1, SSM_CONV_DIM), 1.0),
        'state_ssm': nrm(3, (DEPTH, DEC_BATCH, SSM_N_HEADS, SSM_HEAD_DIM, SSM_D_STATE), 0.1),
        'cache_k_window': nrm(4, (DEPTH, DEC_BATCH, wb, ATTN_N_KV_HEADS, ATTN_HEAD_DIM), 1.0),
        'cache_v_window': nrm(5, (DEPTH, DEC_BATCH, wb, ATTN_N_KV_HEADS, ATTN_HEAD_DIM), 1.0),
        'norm_mix_w': 1.0 + nrm(6, (DEPTH, D_MODEL), 0.05),
        'w_in': nrm(7, (DEPTH, D_MODEL, IN_WIDTH), D_MODEL ** -0.5),
        'conv_w': nrm(8, (DEPTH, SSM_CONV_WIDTH, SSM_CONV_DIM), SSM_CONV_WIDTH ** -0.5),
        'conv_b': nrm(9, (DEPTH, SSM_CONV_DIM), 0.01),
        'dt_bias': dt0 + jnp.log(-jnp.expm1(-dt0)),
        'a_log': jnp.log(jax.random.uniform(ks[11], (DEPTH, SSM_N_HEADS), jnp.float32, 1.0, 16.0)),
        'd_skip': 1.0 + nrm(12, (DEPTH, SSM_N_HEADS), 0.1),
        'ssm_norm_w': 1.0 + nrm(13, (DEPTH, SSM_D_INNER), 0.05),
        'w_ssm_out': nrm(14, (DEPTH, SSM_D_INNER, D_MODEL), SSM_D_INNER ** -0.5),
        'attn_sinks': nrm(15, (DEPTH, ATTN_N_HEADS), 0.5),
        'w_attn_out': nrm(16, (DEPTH, ATTN_Q_WIDTH, D_MODEL), ATTN_Q_WIDTH ** -0.5),
        'w_out': nrm(17, (DEPTH, D_MODEL, D_MODEL), D_MODEL ** -0.5),
        'norm_ffn_w': 1.0 + nrm(18, (DEPTH, D_MODEL), 0.05),
        'peer_w_q': nrm(19, (DEPTH, D_MODEL, PEER_HEADS * PEER_QUERY_DIM), D_MODEL ** -0.5),
        'peer_sub_keys': nrm(20, (DEPTH, 2, PEER_N_KEYS, PEER_HALF_DIM), PEER_HALF_DIM ** -0.5),
        'peer_u': nrm(21, (DEPTH, PEER_N_EXPERTS, D_MODEL), D_MODEL ** -0.5),
        'peer_v': nrm(22, (DEPTH, PEER_N_EXPERTS, D_MODEL), 0.5),
        'norm_final_w': 1.0 + nrm(23, (D_MODEL,), 0.05),
    }


def reference(x_prompt, x_sample, state_conv, state_ssm, cache_k_window, cache_v_window,
              norm_mix_w, w_in, conv_w, conv_b, dt_bias, a_log, d_skip, ssm_norm_w, w_ssm_out,
              attn_sinks, w_attn_out, w_out, norm_ffn_w, peer_w_q, peer_sub_keys, peer_u, peer_v,
              norm_final_w):
    yp, ys = x_prompt, x_sample
    conv_p, ssm_p, k_p, v_p = [], [], [], []
    conv_s, ssm_s, k_s, v_s = [], [], [], []
    for i in range(DEPTH):
        lw = {'norm_mix_w': norm_mix_w[i], 'w_in': w_in[i], 'conv_w': conv_w[i], 'conv_b': conv_b[i],
              'dt_bias': dt_bias[i], 'a_log': a_log[i], 'd_skip': d_skip[i], 'ssm_norm_w': ssm_norm_w[i],
              'w_ssm_out': w_ssm_out[i], 'attn_sinks': attn_sinks[i], 'w_attn_out': w_attn_out[i],
              'w_out': w_out[i], 'norm_ffn_w': norm_ffn_w[i], 'peer_w_q': peer_w_q[i],
              'peer_sub_keys': peer_sub_keys[i], 'peer_u': peer_u[i], 'peer_v': peer_v[i]}
        bp = yp.shape[0]
        conv0 = jnp.zeros((bp, SSM_CONV_WIDTH - 1, SSM_CONV_DIM), yp.dtype)
        ssm0 = jnp.zeros((bp, SSM_N_HEADS, SSM_HEAD_DIM, SSM_D_STATE), yp.dtype)
        yp, c1, s1, k1, v1 = trunk_layer(yp, conv0, ssm0, None, None, 0, lw)
        ys, c2, s2, k2, v2 = trunk_layer(ys, state_conv[i], state_ssm[i], cache_k_window[i], cache_v_window[i], PAST_LEN, lw)
        conv_p.append(c1); ssm_p.append(s1); k_p.append(k1); v_p.append(v1)
        conv_s.append(c2); ssm_s.append(s2); k_s.append(k2); v_s.append(v2)
    y_prompt = rms_norm(yp, norm_final_w)
    y_sample = rms_norm(ys, norm_final_w)
    return (y_prompt, y_sample,
            jnp.stack(conv_p), jnp.stack(ssm_p), jnp.stack(k_p), jnp.stack(v_p),
            jnp.stack(conv_s), jnp.stack(ssm_s), jnp.stack(k_s), jnp.stack(v_s))
```

```python
import functools
import math

import jax
import jax.numpy as jnp
from jax import lax
from jax.experimental import pallas as pl
from jax.experimental.pallas import tpu as pltpu

F32 = jnp.float32
BF16 = jnp.bfloat16
NORM_EPS = 1e-6

D_MODEL = 1024
SSM_HEADS = 32
SSM_HEAD_DIM = 64
SSM_GROUPS = 8
SSM_STATE = 128
SSM_INNER = SSM_HEADS * SSM_HEAD_DIM
SSM_BC = SSM_GROUPS * SSM_STATE
SSM_CONV_DIM = SSM_INNER + 2 * SSM_BC
SSM_CONV_WIDTH = 4
GROUP_COLS = SSM_INNER // SSM_GROUPS
HEADS_PER_GROUP = SSM_HEADS // SSM_GROUPS
CHUNK = 128
HIST = 8
ATTN_HEADS = 16
ATTN_KV_HEADS = 4
ATTN_HEAD_DIM = 64
ATTN_Q_WIDTH = ATTN_HEADS * ATTN_HEAD_DIM
ATTN_KV_WIDTH = ATTN_KV_HEADS * ATTN_HEAD_DIM
Q_PER_KV = ATTN_HEADS // ATTN_KV_HEADS
ROPE_THETA = 10000.0
PEER_KEYS = 128
PEER_HEADS = 8
PEER_HALF = 64
PEER_TOPK = 16
PEER_EXPERTS = PEER_KEYS * PEER_KEYS
OFF_Z = 0
OFF_X = 2048
OFF_BC = 4096
OFF_Q = 6144
OFF_GS = 7168
OFF_GA = 8192
OFF_K = 9216
OFF_V = 9472
OFF_DT = 9728
DT_PAD = 128
PROJ_WIDTH = OFF_DT + DT_PAD
LANES = 128
VMEM_LIMIT = 48 * 1024 * 1024

_NT = (((1,), (1,)), ((), ()))
_TN = (((0,), (0,)), ((), ()))


def _cparams(n_axes):
    return pltpu.CompilerParams(dimension_semantics=("arbitrary",) * n_axes, vmem_limit_bytes=VMEM_LIMIT)


def _rms(x, w):
    ms = jnp.mean(x * x, axis=-1, keepdims=True)
    return (x * lax.rsqrt(ms + NORM_EPS)) * w


def _silu(x):
    return x * jax.nn.sigmoid(x)


def _softplus(x):
    return jnp.maximum(x, 0.0) + jnp.log1p(jnp.exp(-jnp.abs(x)))


def _split3(x):
    hi = x.astype(BF16)
    r = x - hi.astype(F32)
    mid = r.astype(BF16)
    lo = (r - mid.astype(F32)).astype(BF16)
    return hi, mid, lo


def _split2(x):
    hi = x.astype(BF16)
    return hi, (x - hi.astype(F32)).astype(BF16)


def _sel_left(sel, x):
    hi, mid, lo = _split3(x)
    d = lambda p: jnp.dot(sel, p, preferred_element_type=F32)
    return (d(hi) + d(mid)) + d(lo)


def _sel_right(x, sel):
    hi, mid, lo = _split3(x)
    d = lambda p: jnp.dot(p, sel, preferred_element_type=F32)
    return (d(hi) + d(mid)) + d(lo)


def _in_proj_body(x_ref, nw_ref, w_ref, o_ref, h_scr):
    @pl.when(pl.program_id(1) == 0)
    def _():
        h_scr[...] = _rms(x_ref[...], nw_ref[...]).astype(BF16)

    o_ref[...] = jnp.dot(h_scr[...], w_ref[...], preferred_element_type=F32)


def _in_proj(x, norm_w, w_perm):
    t = x.shape[0]
    tm = 1024 if t % 1024 == 0 else 128
    tn = 1408
    return pl.pallas_call(
        _in_proj_body,
        grid=(t // tm, PROJ_WIDTH // tn),
        in_specs=[
            pl.BlockSpec((tm, D_MODEL), lambda i, j: (i, 0)),
            pl.BlockSpec((1, D_MODEL), lambda i, j: (0, 0)),
            pl.BlockSpec((D_MODEL, tn), lambda i, j: (0, j)),
        ],
        out_specs=pl.BlockSpec((tm, tn), lambda i, j: (i, j)),
        out_shape=jax.ShapeDtypeStruct((t, PROJ_WIDTH), F32),
        scratch_shapes=[pltpu.VMEM((tm, D_MODEL), BF16)],
        compiler_params=_cparams(2),
        name="in_proj",
    )(x, norm_w, w_perm)


def _ssd_body(sample, *refs):
    if sample:
        (z_ref, x_ref, bc_ref, dt_ref, hist_ref, s0_ref, cw_ref, cb_ref, dtb_ref, alog_ref, dskip_ref, nw_ref,
         e_ref, et_ref, y_ref, sout_ref, xbuf, u_scr, s_scr, dt_scr) = refs
        nreal = x_ref.shape[0]
        step = pl.program_id(0)

        @pl.when(step == 0)
        def _():
            xbuf[...] = jnp.zeros_like(xbuf)
            dt_scr[...] = jnp.zeros_like(dt_scr)

        xbuf[0:HIST, :] = hist_ref[...]
        xbuf[HIST:HIST + nreal, 0:SSM_INNER] = x_ref[...]
        xbuf[HIST:HIST + nreal, SSM_INNER:SSM_CONV_DIM] = bc_ref[...]
        s_scr[...] = s0_ref[0]
        dt_scr[0:nreal, :] = _softplus(dt_ref[...] + dtb_ref[...])
        dt = dt_scr[...]
    else:
        (z_ref, x_ref, bc_ref, dt_ref, cw_ref, cb_ref, dtb_ref, alog_ref, dskip_ref, nw_ref,
         e_ref, et_ref, y_ref, sout_ref, xbuf, u_scr, s_scr) = refs
        nreal = CHUNK
        step = pl.program_id(1)

        @pl.when(step == 0)
        def _():
            xbuf[0:HIST, :] = jnp.zeros((HIST, SSM_CONV_DIM), F32)
            s_scr[...] = jnp.zeros_like(s_scr)

        xbuf[HIST:HIST + CHUNK, 0:SSM_INNER] = x_ref[...]
        xbuf[HIST:HIST + CHUNK, SSM_INNER:SSM_CONV_DIM] = bc_ref[...]
        dt = _softplus(dt_ref[...] + dtb_ref[...])

    cw_cols = 512
    for cc in range(SSM_CONV_DIM // cw_cols):
        cols = slice(cc * cw_cols, (cc + 1) * cw_cols)
        acc = cb_ref[:, cols] + cw_ref[3:4, cols] * xbuf[HIST:HIST + CHUNK, cols]
        for s in range(1, SSM_CONV_WIDTH):
            acc = acc + cw_ref[3 - s:4 - s, cols] * xbuf[HIST - s:HIST - s + CHUNK, cols]
        u_scr[:, cols] = _silu(acc)
    if not sample:
        xbuf[0:HIST, :] = xbuf[CHUNK:CHUNK + HIST, :]

    ri = lax.broadcasted_iota(jnp.int32, (CHUNK, CHUNK), 0)
    ci = lax.broadcasted_iota(jnp.int32, (CHUNK, CHUNK), 1)
    tri = ci <= ri
    tril = jnp.where(tri, 1.0, 0.0).astype(BF16)

    a = -jnp.exp(alog_ref[...])
    acum = _sel_left(tril, dt * a)
    acum_t = acum.T
    alast = acum[CHUNK - 1:CHUNK, :]
    e_acum = jnp.exp(acum)
    e_to_end = jnp.exp(alast - acum)
    chunk_decay = jnp.broadcast_to(jnp.exp(acum_t[:, CHUNK - 1:CHUNK]), (CHUNK, CHUNK))

    head_of_lane = lax.broadcasted_iota(jnp.int32, (CHUNK, GROUP_COLS), 1) // SSM_HEAD_DIM

    for g in range(SSM_GROUPS):
        cs = slice(g * GROUP_COLS, (g + 1) * GROUP_COLS)
        b_g = u_scr[:, SSM_INNER + g * SSM_STATE:SSM_INNER + (g + 1) * SSM_STATE].astype(BF16)
        c_g = u_scr[:, SSM_INNER + SSM_BC + g * SSM_STATE:SSM_INNER + SSM_BC + (g + 1) * SSM_STATE].astype(BF16)
        xs_g = u_scr[:, cs]
        e_g = e_ref[:, cs]
        xdt = xs_g * _sel_right(dt, e_g)
        xw = (xdt * _sel_right(e_to_end, e_g)).astype(BF16)
        xdt = xdt.astype(BF16)
        cb = lax.dot_general(c_g, b_g, _NT, preferred_element_type=F32)
        y_diag = jnp.zeros((CHUNK, GROUP_COLS), F32)
        for r in range(HEADS_PER_GROUP):
            h = g * HEADS_PER_GROUP + r
            seg = acum[:, h:h + 1] - acum_t[h:h + 1, :]
            decay = jnp.where(tri, jnp.exp(jnp.where(tri, seg, 0.0)), 0.0)
            y_r = jnp.dot((cb * decay).astype(BF16), xdt, preferred_element_type=F32)
            y_diag = jnp.where(head_of_lane == r, y_r, y_diag)
        s_prev = s_scr[cs, :]
        y_off = lax.dot_general(c_g, s_prev.astype(BF16), _NT, preferred_element_type=F32) * _sel_right(e_acum, e_g)
        s_scale = _sel_left(et_ref[cs, :], chunk_decay)
        s_scr[cs, :] = s_prev * s_scale + lax.dot_general(xw, b_g, _TN, preferred_element_type=F32)

        y = (y_diag + y_off + xs_g * dskip_ref[:, cs])[0:nreal, :]
        y = y * _silu(z_ref[:, cs])
        y_ref[:, cs] = _rms(y, nw_ref[:, cs])

    if sample:
        sout_ref[0] = s_scr[...]
    else:
        @pl.when(step == pl.num_programs(1) - 1)
        def _():
            sout_ref[0] = s_scr[...]


def _ssd_consts(conv_w, conv_b, dt_bias, a_log, d_skip, ssm_norm_w):
    pad = lambda v: jnp.pad(v.astype(F32), (0, DT_PAD - SSM_HEADS)).reshape(1, DT_PAD)
    head_of_col = jnp.arange(SSM_INNER) // SSM_HEAD_DIM
    expand = (jnp.arange(DT_PAD)[:, None] == head_of_col[None, :]).astype(BF16)
    return (conv_w.astype(F32), conv_b.astype(F32).reshape(1, SSM_CONV_DIM), pad(dt_bias), pad(a_log),
            jnp.repeat(d_skip.astype(F32), SSM_HEAD_DIM).reshape(1, SSM_INNER),
            ssm_norm_w.astype(F32).reshape(1, SSM_INNER), expand, expand.T)


def _const_specs(consts, n_grid):
    zero = (lambda *_: (0, 0))
    return [pl.BlockSpec(c.shape, zero) for c in consts]


def _ssd_prompt(proj, batch, seq, consts):
    nc = seq // CHUNK
    row = lambda b, c: b * nc + c
    blk = lambda w, col: pl.BlockSpec((CHUNK, w), lambda b, c: (row(b, c), col))
    return pl.pallas_call(
        functools.partial(_ssd_body, False),
        grid=(batch, nc),
        in_specs=[blk(SSM_INNER, OFF_Z // SSM_INNER), blk(SSM_INNER, OFF_X // SSM_INNER),
                  blk(SSM_INNER, OFF_BC // SSM_INNER), blk(DT_PAD, OFF_DT // DT_PAD)] + _const_specs(consts, 2),
        out_specs=[pl.BlockSpec((CHUNK, SSM_INNER), lambda b, c: (row(b, c), 0)),
                   pl.BlockSpec((1, SSM_INNER, SSM_STATE), lambda b, c: (b, 0, 0))],
        out_shape=[jax.ShapeDtypeStruct((batch * seq, SSM_INNER), F32),
                   jax.ShapeDtypeStruct((batch, SSM_INNER, SSM_STATE), F32)],
        scratch_shapes=[pltpu.VMEM((HIST + CHUNK, SSM_CONV_DIM), F32), pltpu.VMEM((CHUNK, SSM_CONV_DIM), F32),
                        pltpu.VMEM((SSM_INNER, SSM_STATE), F32)],
        compiler_params=_cparams(2),
        name="ssd_prompt",
    )(proj, proj, proj, proj, *consts)


def _ssd_sample(proj, row0, n_seq, seq, hist, state, consts):
    base = row0 // seq
    blk = lambda w, col: pl.BlockSpec((seq, w), lambda j: (base + j, col))
    return pl.pallas_call(
        functools.partial(_ssd_body, True),
        grid=(n_seq,),
        in_specs=[blk(SSM_INNER, OFF_Z // SSM_INNER), blk(SSM_INNER, OFF_X // SSM_INNER),
                  blk(SSM_INNER, OFF_BC // SSM_INNER), blk(DT_PAD, OFF_DT // DT_PAD),
                  pl.BlockSpec((HIST, SSM_CONV_DIM), lambda j: (j, 0)),
                  pl.BlockSpec((1, SSM_INNER, SSM_STATE), lambda j: (j, 0, 0))] + _const_specs(consts, 1),
        out_specs=[pl.BlockSpec((seq, SSM_INNER), lambda j: (j, 0)),
                   pl.BlockSpec((1, SSM_INNER, SSM_STATE), lambda j: (j, 0, 0))],
        out_shape=[jax.ShapeDtypeStruct((n_seq * seq, SSM_INNER), F32),
                   jax.ShapeDtypeStruct((n_seq, SSM_INNER, SSM_STATE), F32)],
        scratch_shapes=[pltpu.VMEM((HIST + CHUNK, SSM_CONV_DIM), F32), pltpu.VMEM((CHUNK, SSM_CONV_DIM), F32),
                        pltpu.VMEM((SSM_INNER, SSM_STATE), F32), pltpu.VMEM((CHUNK, DT_PAD), F32)],
        compiler_params=_cparams(1),
        name="ssd_sample",
    )(proj, proj, proj, proj, hist, state, *consts)


def _swa_body(sample, *refs):
    if sample:
        (q_ref, k_ref, v_ref, kc_ref, vc_ref, cos_ref, sin_ref, sink_ref,
         o_ref, kout_ref, vout_ref, kown, vown) = refs
        step = pl.program_id(0)
    else:
        (q_ref, k_ref, v_ref, cos_ref, sin_ref, sink_ref, o_ref, kout_ref, vout_ref, kprev, vprev) = refs
        step = pl.program_id(1)
    nq = q_ref.shape[0]
    cos = cos_ref[...]
    sin = sin_ref[...]
    lane = lax.broadcasted_iota(jnp.int32, (nq, LANES), 1)
    low_half = (lane & (ATTN_HEAD_DIM - 1)) < ATTN_HEAD_DIM // 2
    second_head = lane >= ATTN_HEAD_DIM

    def rope(xb):
        partner = jnp.where(low_half, pltpu.roll(xb, LANES - ATTN_HEAD_DIM // 2, 1),
                            pltpu.roll(xb, ATTN_HEAD_DIM // 2, 1))
        return xb * cos + partner * sin

    k_new = [rope(k_ref[:, m * LANES:(m + 1) * LANES]) for m in range(ATTN_KV_WIDTH // LANES)]
    if sample:
        @pl.when(step == 0)
        def _():
            kown[...] = jnp.zeros_like(kown)
            vown[...] = jnp.zeros_like(vown)

        for m, kb in enumerate(k_new):
            kown[0:nq, m * LANES:(m + 1) * LANES] = kb
        vown[0:nq, :] = v_ref[...]
        k_before = lambda m: kc_ref[0, :, m * LANES:(m + 1) * LANES]
        v_before = lambda m: vc_ref[0, :, m * LANES:(m + 1) * LANES]
        k_own = lambda m: kown[:, m * LANES:(m + 1) * LANES]
        v_own = lambda m: vown[:, m * LANES:(m + 1) * LANES]
    else:
        @pl.when(step == 0)
        def _():
            kprev[...] = jnp.zeros_like(kprev)
            vprev[...] = jnp.zeros_like(vprev)

        k_before = lambda m: kprev[:, m * LANES:(m + 1) * LANES]
        v_before = lambda m: vprev[:, m * LANES:(m + 1) * LANES]
        k_own = lambda m: k_new[m]
        v_own = lambda m: v_ref[:, m * LANES:(m + 1) * LANES]

    qi = lax.broadcasted_iota(jnp.int32, (nq, CHUNK), 0)
    kj = lax.broadcasted_iota(jnp.int32, (nq, CHUNK), 1)
    valid_before = kj > qi
    if not sample:
        valid_before = valid_before & (step > 0)
    valid_own = kj <= qi
    neg = -1e30
    scale = ATTN_HEAD_DIM ** -0.5

    for c in range(ATTN_Q_WIDTH // LANES):
        q_pair = rope(q_ref[:, c * LANES:(c + 1) * LANES])
        kv = (2 * c) // Q_PER_KV
        m = kv // 2
        kv_second = (kv % 2) == 1
        kb = k_before(m).astype(BF16)
        ko = k_own(m).astype(BF16)
        vb = v_before(m).astype(BF16)
        vo = v_own(m).astype(BF16)
        o_pair = jnp.zeros((nq, LANES), F32)
        for hp in range(2):
            sink = sink_ref[2 * c + hp]
            mine = second_head if hp == 1 else jnp.logical_not(second_head)
            qa = jnp.where(mine, q_pair, 0.0)
            if (hp == 1) != kv_second:
                qa = pltpu.roll(qa, ATTN_HEAD_DIM, 1)
            qa = qa.astype(BF16)
            s_b = jnp.where(valid_before, lax.dot_general(qa, kb, _NT, preferred_element_type=F32) * scale, neg)
            s_o = jnp.where(valid_own, lax.dot_general(qa, ko, _NT, preferred_element_type=F32) * scale, neg)
            mx = jnp.maximum(jnp.maximum(jnp.max(s_b, axis=1, keepdims=True), jnp.max(s_o, axis=1, keepdims=True)),
                             sink)
            p_b = jnp.exp(s_b - mx)
            p_o = jnp.exp(s_o - mx)
            denom = (jnp.sum(p_b, axis=1, keepdims=True) + jnp.sum(p_o, axis=1, keepdims=True)
                     + jnp.exp(sink - mx))
            o = (jnp.dot(p_b.astype(BF16), vb, preferred_element_type=F32)
                 + jnp.dot(p_o.astype(BF16), vo, preferred_element_type=F32))
            if (hp == 1) != kv_second:
                o = pltpu.roll(o, ATTN_HEAD_DIM, 1)
            o_pair = jnp.where(mine, o / denom, o_pair)
        o_ref[:, c * LANES:(c + 1) * LANES] = o_pair

    if sample:
        keep = CHUNK - nq
        kout_ref[0, 0:keep, :] = kc_ref[0, nq:CHUNK, :]
        vout_ref[0, 0:keep, :] = vc_ref[0, nq:CHUNK, :]
        for m, kb in enumerate(k_new):
            kout_ref[0, keep:CHUNK, m * LANES:(m + 1) * LANES] = kb
        vout_ref[0, keep:CHUNK, :] = v_ref[...]
    else:
        for m, kb in enumerate(k_new):
            kprev[:, m * LANES:(m + 1) * LANES] = kb
            kout_ref[0, :, m * LANES:(m + 1) * LANES] = kb
        vprev[...] = v_ref[...]
        vout_ref[0] = v_ref[...]


def _rope_tables(pos):
    half = ATTN_HEAD_DIM // 2
    inv = ROPE_THETA ** (-jnp.arange(half, dtype=F32) / half)
    ang = pos.astype(F32)[:, None] * inv[None, :]
    lane = jnp.arange(LANES)
    cos = jnp.cos(ang)[:, lane % half]
    sin = jnp.sin(ang)[:, lane % half]
    sin = jnp.where((lane % ATTN_HEAD_DIM < half)[None, :], -sin, sin)
    return cos, sin


def _swa_prompt(proj, batch, seq, sinks):
    nb = seq // CHUNK
    cos, sin = _rope_tables(jnp.arange(seq))
    row = lambda b, n: b * nb + n
    blk = lambda w, col: pl.BlockSpec((CHUNK, w), lambda b, n: (row(b, n), col))
    tab = pl.BlockSpec((CHUNK, LANES), lambda b, n: (n, 0))
    win = pl.BlockSpec((1, CHUNK, ATTN_KV_WIDTH), lambda b, n: (b, 0, 0))
    return pl.pallas_call(
        functools.partial(_swa_body, False),
        grid=(batch, nb),
        in_specs=[blk(ATTN_Q_WIDTH, OFF_Q // ATTN_Q_WIDTH), blk(ATTN_KV_WIDTH, OFF_K // ATTN_KV_WIDTH),
                  blk(ATTN_KV_WIDTH, OFF_V // ATTN_KV_WIDTH), tab, tab,
                  pl.BlockSpec(memory_space=pltpu.SMEM)],
        out_specs=[pl.BlockSpec((CHUNK, ATTN_Q_WIDTH), lambda b, n: (row(b, n), 0)), win, win],
        out_shape=[jax.ShapeDtypeStruct((batch * seq, ATTN_Q_WIDTH), F32),
                   jax.ShapeDtypeStruct((batch, CHUNK, ATTN_KV_WIDTH), F32),
                   jax.ShapeDtypeStruct((batch, CHUNK, ATTN_KV_WIDTH), F32)],
        scratch_shapes=[pltpu.VMEM((CHUNK, ATTN_KV_WIDTH), F32), pltpu.VMEM((CHUNK, ATTN_KV_WIDTH), F32)],
        compiler_params=_cparams(2),
        name="swa_prompt",
    )(proj, proj, proj, cos, sin, sinks)


def _swa_sample(proj, row0, n_seq, seq, pos0, k_cache, v_cache, sinks):
    cos, sin = _rope_tables(pos0 + jnp.arange(seq))
    base = row0 // seq
    blk = lambda w, col: pl.BlockSpec((seq, w), lambda j: (base + j, col))
    tab = pl.BlockSpec((seq, LANES), lambda j: (0, 0))
    win = pl.BlockSpec((1, CHUNK, ATTN_KV_WIDTH), lambda j: (j, 0, 0))
    return pl.pallas_call(
        functools.partial(_swa_body, True),
        grid=(n_seq,),
        in_specs=[blk(ATTN_Q_WIDTH, OFF_Q // ATTN_Q_WIDTH), blk(ATTN_KV_WIDTH, OFF_K // ATTN_KV_WIDTH),
                  blk(ATTN_KV_WIDTH, OFF_V // ATTN_KV_WIDTH), win, win, tab, tab,
                  pl.BlockSpec(memory_space=pltpu.SMEM)],
        out_specs=[pl.BlockSpec((seq, ATTN_Q_WIDTH), lambda j: (j, 0)), win, win],
        out_shape=[jax.ShapeDtypeStruct((n_seq * seq, ATTN_Q_WIDTH), F32),
                   jax.ShapeDtypeStruct((n_seq, CHUNK, ATTN_KV_WIDTH), F32),
                   jax.ShapeDtypeStruct((n_seq, CHUNK, ATTN_KV_WIDTH), F32)],
        scratch_shapes=[pltpu.VMEM((CHUNK, ATTN_KV_WIDTH), F32), pltpu.VMEM((CHUNK, ATTN_KV_WIDTH), F32)],
        compiler_params=_cparams(1),
        name="swa_sample",
    )(proj, proj, proj, k_cache, v_cache, cos, sin, sinks)


def _mix_body(x_ref, y_ref, o_ref, gs_ref, ga_ref, wso_ref, wao_ref, wo_ref, x2_ref):
    y_ssm = jnp.dot(y_ref[...].astype(BF16), wso_ref[...], preferred_element_type=F32)
    y_attn = jnp.dot(o_ref[...].astype(BF16), wao_ref[...], preferred_element_type=F32)
    merged = jax.nn.sigmoid(gs_ref[...]) * y_ssm + jax.nn.sigmoid(ga_ref[...]) * y_attn
    x2_ref[...] = x_ref[...] + jnp.dot(merged.astype(BF16), wo_ref[...], preferred_element_type=F32)


def _mix(x, y_ssm, o_attn, proj, w_ssm_out, w_attn_out, w_out):
    t = x.shape[0]
    tm = 512 if t % 512 == 0 else 128
    rows = lambda w: pl.BlockSpec((tm, w), lambda i: (i, 0))
    full = lambda a: pl.BlockSpec(a.shape, lambda i: (0, 0))
    return pl.pallas_call(
        _mix_body,
        grid=(t // tm,),
        in_specs=[rows(D_MODEL), rows(SSM_INNER), rows(ATTN_Q_WIDTH),
                  pl.BlockSpec((tm, D_MODEL), lambda i: (i, OFF_GS // D_MODEL)),
                  pl.BlockSpec((tm, D_MODEL), lambda i: (i, OFF_GA // D_MODEL)),
                  full(w_ssm_out), full(w_attn_out), full(w_out)],
        out_specs=rows(D_MODEL),
        out_shape=jax.ShapeDtypeStruct((t, D_MODEL), F32),
        compiler_params=_cparams(1),
        name="mix",
    )(x, y_ssm, o_attn, proj, proj, w_ssm_out, w_attn_out, w_out)


def _top16(cur, n_rows):
    lanes = cur.shape[1]
    rid = lax.broadcasted_iota(jnp.int32, (n_rows, lanes), 0).astype(F32)
    out_id = lax.broadcasted_iota(jnp.int32, (PEER_TOPK, lanes), 0)
    out = jnp.zeros((PEER_TOPK, lanes), F32)
    for k in range(PEER_TOPK):
        mx = jnp.max(cur, axis=0, keepdims=True)
        out = jnp.where(out_id == k, mx, out)
        first = jnp.min(jnp.where(cur == mx, rid, float(n_rows)), axis=0, keepdims=True)
        cur = jnp.where(rid == first, -jnp.inf, cur)
    return out


def _peer_score_body(x2_ref, nfw_ref, wqh_ref, wql_ref, kh_ref, kl_ref, s_ref, st_ref):
    hn = _rms(x2_ref[...], nfw_ref[...])
    hh, hl = _split2(hn)
    wqh = wqh_ref[...]
    q_t = (lax.dot_general(wqh, hh, _NT, preferred_element_type=F32)
           + lax.dot_general(wqh, hl, _NT, preferred_element_type=F32)
           + lax.dot_general(wql_ref[...], hh, _NT, preferred_element_type=F32))
    lanes = q_t.shape[1]
    stat_id = lax.broadcasted_iota(jnp.int32, (8, lanes), 0)
    for h in range(PEER_HEADS):
        s_half = []
        for j in range(2):
            r0 = (2 * h + j) * PEER_HALF
            qh, ql = _split2(q_t[r0:r0 + PEER_HALF, :])
            s = (jnp.dot(kh_ref[j], qh, preferred_element_type=F32) + jnp.dot(kh_ref[j], ql, preferred_element_type=F32)
                 + jnp.dot(kl_ref[j], qh, preferred_element_type=F32))
            s_ref[h, j] = s
            s_half.append(s)
        top0 = _top16(s_half[0], PEER_KEYS)
        top1 = _top16(s_half[1], PEER_KEYS)
        comb = jnp.concatenate([top1 + top0[a:a + 1, :] for a in range(PEER_TOPK)], axis=0)
        best = _top16(comb, PEER_TOPK * PEER_TOPK)
        z = jnp.sum(jnp.exp(best - best[0:1, :]), axis=0, keepdims=True)
        stats = jnp.where(stat_id == 0, best[PEER_TOPK - 1:PEER_TOPK, :], 0.0)
        stats = jnp.where(stat_id == 1, top0[0:1, :], stats)
        stats = jnp.where(stat_id == 2, top1[0:1, :], stats)
        stats = jnp.where(stat_id == 3, 1.0 / z, stats)
        st_ref[h] = stats


def _peer_score(x2, norm_w, wq_t_hi, wq_t_lo, keys_hi, keys_lo):
    t = x2.shape[0]
    tm = LANES
    full = lambda a: pl.BlockSpec(a.shape, lambda i: (0,) * a.ndim)
    return pl.pallas_call(
        _peer_score_body,
        grid=(t // tm,),
        in_specs=[pl.BlockSpec((tm, D_MODEL), lambda i: (i, 0)), full(norm_w), full(wq_t_hi), full(wq_t_lo),
                  full(keys_hi), full(keys_lo)],
        out_specs=[pl.BlockSpec((PEER_HEADS, 2, PEER_KEYS, tm), lambda i: (0, 0, 0, i)),
                   pl.BlockSpec((PEER_HEADS, 8, tm), lambda i: (0, 0, i))],
        out_shape=[jax.ShapeDtypeStruct((PEER_HEADS, 2, PEER_KEYS, t), F32),
                   jax.ShapeDtypeStruct((PEER_HEADS, 8, t), F32)],
        compiler_params=_cparams(1),
        name="peer_score",
    )(x2, norm_w, wq_t_hi, wq_t_lo, keys_hi, keys_lo)


def _peer_body(x2_ref, nfw_ref, s_ref, st_ref, u_ref, vt_ref, fw_ref, y_ref, hn_scr, e1_scr, scl_scr, acc_scr, w_scr):
    e = pl.program_id(1)
    te = u_ref.shape[0]

    @pl.when(e == 0)
    def _():
        hn_scr[...] = _rms(x2_ref[...], nfw_ref[...]).astype(BF16)
        for h in range(PEER_HEADS):
            e1_scr[h] = jnp.exp(s_ref[h, 1] - st_ref[h, 2:3, :])
            scl_scr[h] = jnp.exp(s_ref[h, 0] - st_ref[h, 1:2, :]) * st_ref[h, 3:4, :]
        acc_scr[...] = jnp.zeros_like(acc_scr)

    s_t = lax.dot_general(u_ref[...], hn_scr[...], _NT, preferred_element_type=F32)
    for ii in range(te // PEER_KEYS):
        i = e * (te // PEER_KEYS) + ii
        gate = jnp.zeros((PEER_KEYS, s_t.shape[1]), F32)
        for h in range(PEER_HEADS):
            s0_row = s_ref[h, 0, pl.ds(i, 1), :]
            scl_row = scl_scr[h, pl.ds(i, 1), :]
            picked = (s_ref[h, 1] + s0_row) >= st_ref[h, 0:1, :]
            gate = gate + jnp.where(picked, e1_scr[h] * scl_row, 0.0)
        sb = s_t[ii * PEER_KEYS:(ii + 1) * PEER_KEYS, :]
        act = 0.5 * sb * (1.0 + lax.erf(sb * (0.5 ** 0.5)))
        w_scr[ii * PEER_KEYS:(ii + 1) * PEER_KEYS, :] = (gate * act).astype(BF16)
    acc_scr[...] += jnp.dot(vt_ref[...], w_scr[...], preferred_element_type=F32)

    @pl.when(e == pl.num_programs(1) - 1)
    def _():
        y_ref[...] = _rms(x2_ref[...] + acc_scr[...].T, fw_ref[...])


def _peer(x2, norm_ffn_w, scores, stats, u_bf, vt_bf, norm_final_w):
    t = x2.shape[0]
    tm = 512 if t % 512 == 0 else 128
    te = 512
    return pl.pallas_call(
        _peer_body,
        grid=(t // tm, PEER_EXPERTS // te),
        in_specs=[pl.BlockSpec((tm, D_MODEL), lambda i, e: (i, 0)),
                  pl.BlockSpec((1, D_MODEL), lambda i, e: (0, 0)),
                  pl.BlockSpec((PEER_HEADS, 2, PEER_KEYS, tm), lambda i, e: (0, 0, 0, i)),
                  pl.BlockSpec((PEER_HEADS, 8, tm), lambda i, e: (0, 0, i)),
                  pl.BlockSpec((te, D_MODEL), lambda i, e: (e, 0)),
                  pl.BlockSpec((D_MODEL, te), lambda i, e: (0, e)),
                  pl.BlockSpec((1, D_MODEL), lambda i, e: (0, 0))],
        out_specs=pl.BlockSpec((tm, D_MODEL), lambda i, e: (i, 0)),
        out_shape=jax.ShapeDtypeStruct((t, D_MODEL), F32),
        scratch_shapes=[pltpu.VMEM((tm, D_MODEL), BF16), pltpu.VMEM((PEER_HEADS, PEER_KEYS, tm), F32),
                        pltpu.VMEM((PEER_HEADS, PEER_KEYS, tm), F32), pltpu.VMEM((D_MODEL, tm), F32),
                        pltpu.VMEM((te, tm), BF16)],
        compiler_params=_cparams(2),
        name="peer",
    )(x2, norm_ffn_w, scores, stats, u_bf, vt_bf, norm_final_w)


def _permute_w_in(w_in):
    sizes = (SSM_INNER, SSM_CONV_DIM, SSM_HEADS, ATTN_Q_WIDTH, ATTN_KV_WIDTH, ATTN_KV_WIDTH, D_MODEL, D_MODEL)
    offs = [0]
    for s in sizes:
        offs.append(offs[-1] + s)
    z, xbc, dt, q, k, v, gs, ga = [w_in[:, offs[i]:offs[i + 1]] for i in range(len(sizes))]
    pad = jnp.zeros((w_in.shape[0], DT_PAD - SSM_HEADS), w_in.dtype)
    return jnp.concatenate([z, xbc, q, gs, ga, k, v, dt, pad], axis=1).astype(BF16)


def _layer(x_prompt, x_sample, pos0, state_conv, state_ssm, cache_k, cache_v, lw, norm_final_w):
    batch, seq, _ = x_prompt.shape
    n_seq, dec, _ = x_sample.shape
    tp = batch * seq
    x = jnp.concatenate([x_prompt.reshape(tp, D_MODEL), x_sample.reshape(n_seq * dec, D_MODEL)], axis=0)
    row = lambda v: v.astype(F32).reshape(1, -1)

    proj = _in_proj(x, row(lw['norm_mix_w']), _permute_w_in(lw['w_in']))

    consts = _ssd_consts(lw['conv_w'], lw['conv_b'], lw['dt_bias'], lw['a_log'], lw['d_skip'], lw['ssm_norm_w'])
    y_p, ssm_p = _ssd_prompt(proj, batch, seq, consts)
    hist = jnp.pad(state_conv, ((0, 0), (HIST - (SSM_CONV_WIDTH - 1), 0), (0, 0))).reshape(n_seq * HIST, SSM_CONV_DIM)
    y_s, ssm_s = _ssd_sample(proj, tp, n_seq, dec, hist, state_ssm.reshape(n_seq, SSM_INNER, SSM_STATE), consts)

    sinks = lw['attn_sinks'].astype(F32)
    o_p, k_p, v_p = _swa_prompt(proj, batch, seq, sinks)
    o_s, k_s, v_s = _swa_sample(proj, tp, n_seq, dec, pos0, cache_k.reshape(n_seq, CHUNK, ATTN_KV_WIDTH),
                                cache_v.reshape(n_seq, CHUNK, ATTN_KV_WIDTH), sinks)

    x2 = _mix(x, jnp.concatenate([y_p, y_s], axis=0), jnp.concatenate([o_p, o_s], axis=0), proj,
              lw['w_ssm_out'].astype(BF16), lw['w_attn_out'].astype(BF16), lw['w_out'].astype(BF16))

    wq_t = lw['peer_w_q'].astype(F32).T
    wq_hi = wq_t.astype(BF16)
    wq_lo = (wq_t - wq_hi.astype(F32)).astype(BF16)
    keys = lw['peer_sub_keys'].astype(F32)
    keys_hi = keys.astype(BF16)
    keys_lo = (keys - keys_hi.astype(F32)).astype(BF16)
    scores, stats = _peer_score(x2, row(lw['norm_ffn_w']), wq_hi, wq_lo, keys_hi, keys_lo)
    y = _peer(x2, row(lw['norm_ffn_w']), scores, stats, lw['peer_u'].astype(BF16), lw['peer_v'].astype(BF16).T,
              row(norm_final_w))

    conv_p = proj[:tp, OFF_X:OFF_X + SSM_CONV_DIM].reshape(batch, seq, SSM_CONV_DIM)[:, seq - 3:, :]
    conv_s = proj[tp:, OFF_X:OFF_X + SSM_CONV_DIM].reshape(n_seq, dec, SSM_CONV_DIM)[:, dec - 3:, :]
    state_shape = (SSM_HEADS, SSM_HEAD_DIM, SSM_STATE)
    win_shape = (CHUNK, ATTN_KV_HEADS, ATTN_HEAD_DIM)
    return (y[:tp].reshape(batch, seq, D_MODEL), y[tp:].reshape(n_seq, dec, D_MODEL),
            conv_p, ssm_p.reshape(batch, *state_shape), k_p.reshape(batch, *win_shape), v_p.reshape(batch, *win_shape),
            conv_s, ssm_s.reshape(n_seq, *state_shape), k_s.reshape(n_seq, *win_shape), v_s.reshape(n_seq, *win_shape))


def kernel(x_prompt, x_sample, state_conv, state_ssm, cache_k_window, cache_v_window, norm_mix_w, w_in, conv_w, conv_b, dt_bias, a_log, d_skip, ssm_norm_w, w_ssm_out, attn_sinks, w_attn_out, w_out, norm_ffn_w, peer_w_q, peer_sub_keys, peer_u, peer_v, norm_final_w):
    assert w_in.shape[0] == 1, "single-layer trunk"
    lw = {'norm_mix_w': norm_mix_w[0], 'w_in': w_in[0], 'conv_w': conv_w[0], 'conv_b': conv_b[0],
          'dt_bias': dt_bias[0], 'a_log': a_log[0], 'd_skip': d_skip[0], 'ssm_norm_w': ssm_norm_w[0],
          'w_ssm_out': w_ssm_out[0], 'attn_sinks': attn_sinks[0], 'w_attn_out': w_attn_out[0], 'w_out': w_out[0],
          'norm_ffn_w': norm_ffn_w[0], 'peer_w_q': peer_w_q[0], 'peer_sub_keys': peer_sub_keys[0],
          'peer_u': peer_u[0], 'peer_v': peer_v[0]}
    past_len = 8192
    outs = _layer(x_prompt, x_sample, past_len, state_conv[0], state_ssm[0], cache_k_window[0], cache_v_window[0],
                  lw, norm_final_w)
    yp, ys = outs[0], outs[1]
    return (yp, ys) + tuple(o[None] for o in outs[2:])
```
